```python
import math
import jax
import jax.numpy as jnp
from jax import lax
import numpy as np

D_MODEL = 1024
BATCH = 32
SEQ = 256
DEPTH = 4
DEC_BATCH = 4
DEC_SEQ = 1024
PAST_LEN = 256

GRID_W = 64
D_MIX = D_MODEL
MIX_W = D_MIX // 4
CHUNK = 128
EPS = 1e-6
SSD_HEAD_DIM = 64
SSD_HEADS = MIX_W // SSD_HEAD_DIM
SSD_GROUPS = 2
SSD_STATE = 64
SSD_CONV = 4
SSD_XBC = MIX_W + 2 * SSD_GROUPS * SSD_STATE
SSD_COLS = MIX_W + SSD_XBC + 2 * SSD_HEADS
RET_HEADS = 4
RET_HEAD_DIM = MIX_W // RET_HEADS
ROPE_BASE = 10000.0
RET_COLS = 4 * MIX_W
S5_GROUP_CH = 16
S5_GROUPS = MIX_W // S5_GROUP_CH
S5_STATE = 64
S5_COLS = MIX_W
LRU_BLOCKS = 4
LRU_BLOCK = MIX_W // LRU_BLOCKS
LRU_CONV = 4
LRU_C = 8.0
LRU_COLS = 2 * MIX_W
IN_COLS = SSD_COLS + RET_COLS + S5_COLS + LRU_COLS
D_FF = ((8 * D_MODEL // 3 + 127) // 128) * 128
FFN_CONV = 3

kernel_name = 'hybrid_diffusion_parallel_ssm_step'


def rmsnorm(x, w):
    xf = x.astype(jnp.float32)
    return xf * lax.rsqrt(jnp.mean(xf * xf, axis=-1, keepdims=True) + EPS) * w.astype(jnp.float32)


def dwconv(x, w, b):
    k = w.shape[0]
    y = lax.conv_general_dilated(x, w[:, None, :].astype(x.dtype), (1,), [(k // 2, k - 1 - k // 2)],
                                 dimension_numbers=('NWC', 'WIO', 'NWC'), feature_group_count=x.shape[-1])
    return y + b


def flip(t):
    return jnp.flip(t, axis=1)


def chunked_decay_scan(q, k, v, log_a, s0):
    b, L, h, n = q.shape
    p = v.shape[-1]
    nc = L // CHUNK
    f32 = jnp.float32
    qc = q.astype(f32).reshape(b, nc, CHUNK, h, n)
    kc = k.astype(f32).reshape(b, nc, CHUNK, h, n)
    vc = v.astype(f32).reshape(b, nc, CHUNK, h, p)
    cum = jnp.cumsum(log_a.astype(f32).reshape(b, nc, CHUNK, h), axis=2)
    lower = jnp.tril(jnp.ones((CHUNK, CHUNK), bool))[None, None, :, :, None]
    seg = cum[:, :, :, None, :] - cum[:, :, None, :, :]
    decay = jnp.where(lower, jnp.exp(jnp.where(lower, seg, 0.0)), 0.0)
    scores = jnp.einsum('bclhn,bcshn->bclsh', qc, kc) * decay
    y = jnp.einsum('bclsh,bcshp->bclhp', scores, vc)
    tail = jnp.exp(cum[:, :, -1:, :] - cum)
    chunk_states = jnp.einsum('bcshn,bcsh,bcshp->bchnp', kc, tail, vc)
    chunk_decay = jnp.exp(cum[:, :, -1, :])

    def step(s, inp):
        cs, cd = inp
        return cd[..., None, None] * s + cs, s

    s_final, s_enter = lax.scan(step, s0.astype(f32),
                                (jnp.moveaxis(chunk_states, 1, 0), jnp.moveaxis(chunk_decay, 1, 0)))
    s_enter = jnp.moveaxis(s_enter, 0, 1)
    y = y + jnp.einsum('bclhn,bchnp,bclh->bclhp', qc, s_enter, jnp.exp(cum))
    return y.reshape(b, L, h, p), s_final


def diag_scan(a, u, s0):
    def comb(e1, e2):
        a1, b1 = e1
        a2, b2 = e2
        return a1 * a2, a2 * b1 + b2
    a_cum, b_cum = lax.associative_scan(comb, (a, u), axis=1)
    h = b_cum + a_cum * s0[:, None]
    return h, h[:, -1]


def complex_diag_scan(a_re, a_im, u_re, u_im, s_re, s_im):
    def comb(e1, e2):
        ar1, ai1, br1, bi1 = e1
        ar2, ai2, br2, bi2 = e2
        return (ar2 * ar1 - ai2 * ai1, ar2 * ai1 + ai2 * ar1,
                ar2 * br1 - ai2 * bi1 + br2, ar2 * bi1 + ai2 * br1 + bi2)
    ar, ai, br, bi = lax.associative_scan(comb, (a_re, a_im, u_re, u_im), axis=1)
    s_re = s_re[:, None]
    s_im = s_im[:, None]
    return br + ar * s_re - ai * s_im, bi + ar * s_im + ai * s_re


def grid_rotary(t):
    L = t.shape[1]
    rows = L // GRID_W
    row = jnp.repeat(jnp.arange(rows, dtype=jnp.float32), GRID_W)
    col = jnp.tile(jnp.arange(GRID_W, dtype=jnp.float32), rows)
    nf = RET_HEAD_DIM // 4
    inv = ROPE_BASE ** (-jnp.arange(nf, dtype=jnp.float32) / nf)
    ang = jnp.stack([row, col], axis=-1)[:, :, None] * inv
    cos = jnp.cos(ang)[None, :, None]
    sin = jnp.sin(ang)[None, :, None]
    tr = t.reshape(t.shape[:3] + (2, 2, nf))
    re, im = tr[..., 0, :], tr[..., 1, :]
    out = jnp.stack([re * cos - im * sin, re * sin + im * cos], axis=-2)
    return out.reshape(t.shape)


def retention_log_decay(offset):
    return jnp.log1p(-2.0 ** (-5.0 - offset - jnp.arange(RET_HEADS, dtype=jnp.float32)))


def ssd_mixer(p, init, conv_w, conv_b, dt_bias, a_log, d_skip, norm_w):
    b, L, _ = p.shape
    z, xbc, dt = jnp.split(p, [MIX_W, MIX_W + SSD_XBC], axis=-1)
    xbc = jax.nn.silu(dwconv(xbc, conv_w, conv_b))
    x, bm, cm = jnp.split(xbc, [MIX_W, MIX_W + SSD_GROUPS * SSD_STATE], axis=-1)
    x = x.reshape(b, L, SSD_HEADS, SSD_HEAD_DIM)
    rep = SSD_HEADS // SSD_GROUPS
    bm = jnp.repeat(bm.reshape(b, L, SSD_GROUPS, SSD_STATE), rep, axis=2)
    cm = jnp.repeat(cm.reshape(b, L, SSD_GROUPS, SSD_STATE), rep, axis=2)
    dt = jax.nn.softplus(dt.reshape(b, L, 2, SSD_HEADS) + dt_bias)
    a = -jnp.exp(a_log.astype(jnp.float32))
    y_f, s_f = chunked_decay_scan(cm, bm, x * dt[:, :, 0, :, None], dt[:, :, 0] * a[0], init[:, 0])
    y_b, s_b = chunked_decay_scan(flip(cm), flip(bm), flip(x * dt[:, :, 1, :, None]),
                                  flip(dt[:, :, 1] * a[1]), init[:, 1])
    y = y_f + flip(y_b) + d_skip[:, None] * x
    y = y.reshape(b, L, MIX_W) * jax.nn.silu(z)
    return rmsnorm(y, norm_w), jnp.stack([s_f, s_b], axis=1)


def retention_mixer(p, init, gn_w, rotary):
    b, L, _ = p.shape
    q, k, v, g = jnp.split(p, 4, axis=-1)
    shp = (b, L, RET_HEADS, RET_HEAD_DIM)
    q, k, v = q.reshape(shp), k.reshape(shp), v.reshape(shp)
    if rotary:
        q, k = grid_rotary(q), grid_rotary(k)
    k = k * RET_HEAD_DIM ** -0.5
    lg_f = jnp.broadcast_to(retention_log_decay(0.0), (b, L, RET_HEADS))
    lg_b = jnp.broadcast_to(retention_log_decay(0.5), (b, L, RET_HEADS))
    y_f, s_f = chunked_decay_scan(q, k, v, lg_f, init[:, 0])
    y_b, s_b = chunked_decay_scan(flip(q), flip(k), flip(v), lg_b, init[:, 1])
    y = y_f + flip(y_b)
    mu = jnp.mean(y, axis=-1, keepdims=True)
    var = jnp.mean(jnp.square(y - mu), axis=-1, keepdims=True)
    y = ((y - mu) * lax.rsqrt(var + EPS)).reshape(b, L, MIX_W) * gn_w
    return jax.nn.silu(g) * y, jnp.stack([s_f, s_b], axis=1)


def s5_mixer(u, init, lam_re, lam_im, log_step, b_re, b_im, c_re, c_im, d_skip, glu_w):
    f32 = jnp.float32
    bsz, L, _ = u.shape
    ug = u.reshape(bsz, L, S5_GROUPS, S5_GROUP_CH)
    b_re, b_im, c_re, c_im = b_re.astype(f32), b_im.astype(f32), c_re.astype(f32), c_im.astype(f32)

    def direction(ud, dr):
        step = jnp.exp(log_step[dr].astype(f32))[:, None]
        lr, li = lam_re[dr].astype(f32), lam_im[dr].astype(f32)
        mag = jnp.exp(lr * step)
        abar_re, abar_im = mag * jnp.cos(li * step), mag * jnp.sin(li * step)
        den = lr * lr + li * li
        nr, ni = abar_re - 1.0, abar_im
        coef_re = ((nr * lr + ni * li) / den)[..., None]
        coef_im = ((ni * lr - nr * li) / den)[..., None]
        bbar_re = coef_re * b_re - coef_im * b_im
        bbar_im = coef_re * b_im + coef_im * b_re
        bu_re = jnp.einsum('blgm,gnm->blgn', ud, bbar_re)
        bu_im = jnp.einsum('blgm,gnm->blgn', ud, bbar_im)
        x_re, x_im = complex_diag_scan(jnp.broadcast_to(abar_re, bu_re.shape), jnp.broadcast_to(abar_im, bu_im.shape),
                                       bu_re, bu_im, init[:, dr, :, :, 0], init[:, dr, :, :, 1])
        y = jnp.einsum('blgn,gmn->blgm', x_re, c_re) - jnp.einsum('blgn,gmn->blgm', x_im, c_im)
        return y, jnp.stack([x_re[:, -1], x_im[:, -1]], axis=-1)

    y_f, s_f = direction(ug, 0)
    y_b, s_b = direction(flip(ug), 1)
    y = (y_f + flip(y_b)).reshape(bsz, L, MIX_W) + d_skip * u
    val, gate = jnp.split(y @ glu_w, 2, axis=-1)
    return val * jax.nn.sigmoid(gate), jnp.stack([s_f, s_b], axis=1)


def lru_mixer(p, init, conv_w, conv_b, w_a, b_a, w_x, b_x, lam):
    xb, gb = jnp.split(p, 2, axis=-1)
    xb = dwconv(xb, conv_w, conv_b)
    bsz, L, _ = xb.shape

    def direction(xd, dr):
        xr = xd.reshape(bsz, L, LRU_BLOCKS, LRU_BLOCK)
        r = jax.nn.sigmoid(jnp.einsum('blhi,hij->blhj', xr, w_a[dr]) + b_a[dr])
        i = jax.nn.sigmoid(jnp.einsum('blhi,hij->blhj', xr, w_x[dr]) + b_x[dr])
        log_a = -LRU_C * r * jax.nn.softplus(-lam[dr].astype(jnp.float32).reshape(LRU_BLOCKS, LRU_BLOCK))
        a = jnp.exp(log_a)
        u = jnp.sqrt(-jnp.expm1(2.0 * log_a)) * (i * xr)
        return diag_scan(a.reshape(bsz, L, MIX_W), u.reshape(bsz, L, MIX_W), init[:, dr])

    h_f, s_f = direction(xb, 0)
    h_b, s_b = direction(flip(xb), 1)
    y = (h_f + flip(h_b)) * jax.nn.gelu(gb)
    return y, jnp.stack([s_f, s_b], axis=1)


def conv_ffn(h, w_up, conv_w, conv_b, w_down):
    g, u = jnp.split(h @ w_up, 2, axis=-1)
    g = dwconv(g, conv_w, conv_b)
    return (jax.nn.silu(g) * u) @ w_down


def trunk_layer(x, cond, init, lp, rotary):
    mod = jax.nn.silu(cond.astype(jnp.float32)) @ lp['mod_w'] + lp['mod_b']
    sh1, sc1, g1, sh2, sc2, g2 = jnp.split(mod[:, None, :], 6, axis=-1)
    h = rmsnorm(x, lp['norm1_w']) * (1.0 + sc1) + sh1
    proj = h @ lp['w_in']
    o1 = SSD_COLS
    o2 = o1 + RET_COLS
    o3 = o2 + S5_COLS
    pa, pb, pc, pd = jnp.split(proj, [o1, o2, o3], axis=-1)
    ya, st_a = ssd_mixer(pa, init[0], lp['ssd_conv_w'], lp['ssd_conv_b'], lp['ssd_dt_bias'],
                         lp['ssd_a_log'], lp['ssd_d'], lp['ssd_norm_w'])
    yb, st_b = retention_mixer(pb, init[1], lp['ret_gn_w'], rotary)
    yc, st_c = s5_mixer(pc, init[2], lp['s5_lam_re'], lp['s5_lam_im'], lp['s5_log_step'], lp['s5_b_re'],
                        lp['s5_b_im'], lp['s5_c_re'], lp['s5_c_im'], lp['s5_d'], lp['s5_glu_w'])
    yd, st_d = lru_mixer(pd, init[3], lp['lru_conv_w'], lp['lru_conv_b'], lp['lru_wa'], lp['lru_ba'],
                         lp['lru_wx'], lp['lru_bx'], lp['lru_lambda'])
    y = jnp.concatenate([ya, yb, yc, yd], axis=-1) @ lp['w_out']
    x = x + g1 * y
    h = rmsnorm(x, lp['norm2_w']) * (1.0 + sc2) + sh2
    x = x + g2 * conv_ffn(h, lp['ffn_w_up'], lp['ffn_conv_w'], lp['ffn_conv_b'], lp['ffn_w_down'])
    return x, (st_a, st_b, st_c, st_d)


def setup_inputs(seed: int = 0) -> dict:
    key = jax.random.key(seed)
    ks = iter(jax.random.split(key, 48))
    f32 = jnp.float32

    def nrm(shape, scale):
        return jax.random.normal(next(ks), shape, f32) * scale

    def unif(shape, lo, hi):
        return jax.random.uniform(next(ks), shape, f32, minval=lo, maxval=hi)

    inp = {}
    inp['x_prompt'] = nrm((BATCH, SEQ, D_MODEL), 1.0)
    inp['x_sample'] = nrm((DEC_BATCH, DEC_SEQ, D_MODEL), 1.0)
    inp['state_ssd'] = nrm((DEC_BATCH, DEPTH, 2, SSD_HEADS, SSD_STATE, SSD_HEAD_DIM), 0.3)
    inp['state_ret'] = nrm((DEC_BATCH, DEPTH, 2, RET_HEADS, RET_HEAD_DIM, RET_HEAD_DIM), 0.3)
    inp['state_s5'] = nrm((DEC_BATCH, DEPTH, 2, S5_GROUPS, S5_STATE, 2), 0.1)
    inp['state_lru'] = nrm((DEC_BATCH, DEPTH, 2, MIX_W), 0.5)
    inp['c'] = nrm((DEC_BATCH, D_MODEL), 1.0)
    inp['c_ctx'] = nrm((D_MODEL,), 1.0)
    inp['norm1_w'] = 1.0 + nrm((DEPTH, D_MODEL), 0.02)
    inp['norm2_w'] = 1.0 + nrm((DEPTH, D_MODEL), 0.02)
    inp['mod_w'] = nrm((DEPTH, D_MODEL, 6 * D_MODEL), 0.5 * D_MODEL ** -0.5)
    inp['mod_b'] = nrm((DEPTH, 6 * D_MODEL), 0.02)
    inp['w_in'] = nrm((DEPTH, D_MODEL, IN_COLS), D_MODEL ** -0.5)
    inp['ssd_conv_w'] = nrm((DEPTH, SSD_CONV, SSD_XBC), SSD_CONV ** -0.5)
    inp['ssd_conv_b'] = nrm((DEPTH, SSD_XBC), 0.02)
    dt0 = jnp.exp(unif((DEPTH, 2, SSD_HEADS), math.log(1e-3), math.log(1e-1)))
    inp['ssd_dt_bias'] = dt0 + jnp.log(-jnp.expm1(-dt0))
    inp['ssd_a_log'] = jnp.log(unif((DEPTH, 2, SSD_HEADS), 1.0, 16.0))
    inp['ssd_d'] = 1.0 + nrm((DEPTH, SSD_HEADS), 0.02)
    inp['ssd_norm_w'] = 1.0 + nrm((DEPTH, MIX_W), 0.02)
    inp['ret_gn_w'] = 1.0 + nrm((DEPTH, MIX_W), 0.02)
    inp['s5_lam_re'] = -0.5 + nrm((DEPTH, 2, S5_GROUPS, S5_STATE), 0.01)
    inp['s5_lam_im'] = math.pi * jnp.arange(S5_STATE, dtype=f32) + nrm((DEPTH, 2, S5_GROUPS, S5_STATE), 0.01)
    inp['s5_log_step'] = unif((DEPTH, 2, S5_GROUPS), math.log(1e-3), math.log(1e-1))
    inp['s5_b_re'] = nrm((DEPTH, S5_GROUPS, S5_STATE, S5_GROUP_CH), (2 * S5_GROUP_CH) ** -0.5)
    inp['s5_b_im'] = nrm((DEPTH, S5_GROUPS, S5_STATE, S5_GROUP_CH), (2 * S5_GROUP_CH) ** -0.5)
    inp['s5_c_re'] = nrm((DEPTH, S5_GROUPS, S5_GROUP_CH, S5_STATE), S5_STATE ** -0.5)
    inp['s5_c_im'] = nrm((DEPTH, S5_GROUPS, S5_GROUP_CH, S5_STATE), S5_STATE ** -0.5)
    inp['s5_d'] = nrm((DEPTH, MIX_W), 1.0)
    inp['s5_glu_w'] = nrm((DEPTH, MIX_W, 2 * MIX_W), MIX_W ** -0.5)
    inp['lru_conv_w'] = nrm((DEPTH, LRU_CONV, MIX_W), LRU_CONV ** -0.5)
    inp['lru_conv_b'] = nrm((DEPTH, MIX_W), 0.02)
    inp['lru_wa'] = nrm((DEPTH, 2, LRU_BLOCKS, LRU_BLOCK, LRU_BLOCK), LRU_BLOCK ** -0.5)
    inp['lru_ba'] = nrm((DEPTH, 2, LRU_BLOCKS, LRU_BLOCK), 0.02)
    inp['lru_wx'] = nrm((DEPTH, 2, LRU_BLOCKS, LRU_BLOCK, LRU_BLOCK), LRU_BLOCK ** -0.5)
    inp['lru_bx'] = nrm((DEPTH, 2, LRU_BLOCKS, LRU_BLOCK), 0.02)
    sig = unif((DEPTH, 2, MIX_W), 0.9, 0.999) ** (1.0 / LRU_C)
    inp['lru_lambda'] = jnp.log(sig) - jnp.log1p(-sig)
    inp['w_out'] = nrm((DEPTH, D_MIX, D_MODEL), D_MIX ** -0.5)
    inp['ffn_w_up'] = nrm((DEPTH, D_MODEL, 2 * D_FF), D_MODEL ** -0.5)
    inp['ffn_conv_w'] = nrm((DEPTH, FFN_CONV, D_FF), FFN_CONV ** -0.5)
    inp['ffn_conv_b'] = nrm((DEPTH, D_FF), 0.02)
    inp['ffn_w_down'] = nrm((DEPTH, D_FF, D_MODEL), D_FF ** -0.5)
    inp['final_norm_w'] = 1.0 + nrm((D_MODEL,), 0.02)
    return inp


def reference(x_prompt, x_sample, state_ssd, state_ret, state_s5, state_lru, c, c_ctx,
              norm1_w, norm2_w, mod_w, mod_b, w_in, ssd_conv_w, ssd_conv_b, ssd_dt_bias, ssd_a_log,
              ssd_d, ssd_norm_w, ret_gn_w, s5_lam_re, s5_lam_im, s5_log_step, s5_b_re, s5_b_im,
              s5_c_re, s5_c_im, s5_d, s5_glu_w, lru_conv_w, lru_conv_b, lru_wa, lru_ba, lru_wx, lru_bx,
              lru_lambda, w_out, ffn_w_up, ffn_conv_w, ffn_conv_b, ffn_w_down, final_norm_w):
    f32 = jnp.float32
    bp = x_prompt.shape[0]
    xp = x_prompt.astype(f32)
    xs = x_sample.astype(f32)
    zero_init = (jnp.zeros((bp, 2, SSD_HEADS, SSD_STATE, SSD_HEAD_DIM), f32),
                 jnp.zeros((bp, 2, RET_HEADS, RET_HEAD_DIM, RET_HEAD_DIM), f32),
                 jnp.zeros((bp, 2, S5_GROUPS, S5_STATE, 2), f32),
                 jnp.zeros((bp, 2, MIX_W), f32))
    new_ssd, new_ret, new_s5, new_lru = [], [], [], []
    for l in range(DEPTH):
        lp = {'norm1_w': norm1_w[l], 'norm2_w': norm2_w[l], 'mod_w': mod_w[l], 'mod_b': mod_b[l],
              'w_in': w_in[l], 'ssd_conv_w': ssd_conv_w[l], 'ssd_conv_b': ssd_conv_b[l],
              'ssd_dt_bias': ssd_dt_bias[l], 'ssd_a_log': ssd_a_log[l], 'ssd_d': ssd_d[l],
              'ssd_norm_w': ssd_norm_w[l], 'ret_gn_w': ret_gn_w[l], 's5_lam_re': s5_lam_re[l],
              's5_lam_im': s5_lam_im[l], 's5_log_step': s5_log_step[l], 's5_b_re': s5_b_re[l],
              's5_b_im': s5_b_im[l], 's5_c_re': s5_c_re[l], 's5_c_im': s5_c_im[l], 's5_d': s5_d[l],
              's5_glu_w': s5_glu_w[l], 'lru_conv_w': lru_conv_w[l], 'lru_conv_b': lru_conv_b[l],
              'lru_wa': lru_wa[l], 'lru_ba': lru_ba[l], 'lru_wx': lru_wx[l], 'lru_bx': lru_bx[l],
              'lru_lambda': lru_lambda[l], 'w_out': w_out[l], 'ffn_w_up': ffn_w_up[l],
              'ffn_conv_w': ffn_conv_w[l], 'ffn_conv_b': ffn_conv_b[l], 'ffn_w_down': ffn_w_down[l]}
        xp, st = trunk_layer(xp, c_ctx[None, :], zero_init, lp, False)
        new_ssd.append(st[0])
        new_ret.append(st[1])
        new_s5.append(st[2])
        new_lru.append(st[3])
        cache = (state_ssd[:, l], state_ret[:, l], state_s5[:, l], state_lru[:, l])
        xs, _ = trunk_layer(xs, c, cache, lp, True)
    y_prompt = rmsnorm(xp, final_norm_w)
    y_sample = rmsnorm(xs, final_norm_w)
    new_state_ssd = jnp.stack(new_ssd, axis=1)
    new_state_ret = jnp.stack(new_ret, axis=1)
    new_state_s5 = jnp.stack(new_s5, axis=1)
    new_state_lru = jnp.stack(new_lru, axis=1)
    return (y_prompt, y_sample, new_state_ssd, new_state_ret, new_state_s5, new_state_lru)
```

```python
import functools
import math

import jax
import jax.numpy as jnp
from jax import lax
from jax.experimental import pallas as pl
from jax.experimental.pallas import tpu as pltpu

F32 = jnp.float32
BF16 = jnp.bfloat16
HIGHEST = lax.Precision.HIGHEST

D_MODEL = 1024
DEPTH = 4
GRID_W = 64
MIX_W = 256
EPS = 1e-6
SSD_HEADS = 4
SSD_HEAD_DIM = 64
SSD_STATE = 64
SSD_XBC = 512
RET_HEADS = 4
RET_HEAD_DIM = 64
ROPE_BASE = 10000.0
S5_GROUPS = 16
S5_GROUP_CH = 16
S5_STATE = 64
S5_W = S5_GROUPS * S5_STATE
LRU_BLOCKS = 4
LRU_BLOCK = 64
LRU_C = 8.0
D_FF = 2816
N_COND = 8
CTX_ROW = 4

OFF_Z, OFF_XBC, OFF_Q, OFF_K, OFF_V, OFF_G = 0, 256, 768, 1024, 1280, 1536
OFF_S5, OFF_LXB, OFF_LGB, OFF_DT = 1792, 2048, 2304, 2560
IN_PAD = 2688

T = 128
SUB = 8
FF_CHUNK = 256
N_FF = D_FF // FF_CHUNK
VMEM_LIMIT = 56 * 1024 * 1024


def _silu(x):
    return x * jax.nn.sigmoid(x)


def _softplus(x):
    return jnp.maximum(x, 0.0) + jnp.log1p(jnp.exp(-jnp.abs(x)))


def _dot(a, b):
    return jnp.dot(a, b, preferred_element_type=F32)


def _dot_nt(a, b):
    return lax.dot_general(a, b, (((1,), (1,)), ((), ())), preferred_element_type=F32)


def _dot_tn(a, b):
    return lax.dot_general(a, b, (((0,), (0,)), ((), ())), preferred_element_type=F32)


def _shift_rows(x, s, period):
    n = x.shape[0]
    pos = lax.broadcasted_iota(jnp.int32, x.shape, 0) % period
    if s > 0:
        return jnp.where(pos >= s, pltpu.roll(x, s, axis=0), 0.0)
    return jnp.where(pos < period + s, pltpu.roll(x, n + s, axis=0), 0.0)


def _mod_kernel(cond_ref, w_ref, b_ref, o_ref):
    s = _silu(cond_ref[...])
    o_ref[0] = jnp.dot(s, w_ref[0], preferred_element_type=F32, precision=HIGHEST) + b_ref[0]


def _mod_call(cond, mod_w, mod_b):
    tn = 1536
    return pl.pallas_call(
        _mod_kernel,
        grid=(DEPTH, 6 * D_MODEL // tn),
        in_specs=[
            pl.BlockSpec((N_COND, D_MODEL), lambda l, j: (0, 0)),
            pl.BlockSpec((1, D_MODEL, tn), lambda l, j: (l, 0, j)),
            pl.BlockSpec((1, 1, tn), lambda l, j: (l, 0, j)),
        ],
        out_specs=pl.BlockSpec((1, N_COND, tn), lambda l, j: (l, 0, j)),
        out_shape=jax.ShapeDtypeStruct((DEPTH, N_COND, 6 * D_MODEL), F32),
        compiler_params=pltpu.CompilerParams(
            dimension_semantics=("arbitrary", "arbitrary"), vmem_limit_bytes=VMEM_LIMIT),
        name="adaln_mod",
    )(cond, mod_w, mod_b.reshape(DEPTH, 1, 6 * D_MODEL))


def _inproj_kernel(x_ref, mod_ref, nw_ref, w_ref, o_ref):
    x = x_ref[...]
    mod = mod_ref[0]
    sh1 = mod[:, 0:D_MODEL]
    sc1 = mod[:, D_MODEL:2 * D_MODEL]
    ms = jnp.mean(x * x, axis=-1, keepdims=True)
    h = x * lax.rsqrt(ms + EPS) * nw_ref[...] * (1.0 + sc1) + sh1
    o_ref[...] = _dot(h.astype(BF16), w_ref[...])


def _inproj_call(x, mod3, norm_w, w_in, *, tile, cond_base, tiles_per_cond):
    rows = x.shape[0]
    return pl.pallas_call(
        _inproj_kernel,
        grid=(rows // tile,),
        in_specs=[
            pl.BlockSpec((tile, D_MODEL), lambda i: (i, 0)),
            pl.BlockSpec((1, 1, 6 * D_MODEL), lambda i: (cond_base + i // tiles_per_cond, 0, 0)),
            pl.BlockSpec((1, D_MODEL), lambda i: (0, 0)),
            pl.BlockSpec((D_MODEL, IN_PAD), lambda i: (0, 0)),
        ],
        out_specs=pl.BlockSpec((tile, IN_PAD), lambda i: (i, 0)),
        out_shape=jax.ShapeDtypeStruct((rows, IN_PAD), F32),
        compiler_params=pltpu.CompilerParams(
            dimension_semantics=("arbitrary",), vmem_limit_bytes=VMEM_LIMIT),
        name="in_proj",
    )(x, mod3, norm_w, w_in)


def _outffn_kernel(x_ref, y_ref, mod_ref, n2_ref, wo_ref, wg_ref, wu_ref, cw_ref, cb_ref, wd_ref,
                   fw_ref, o_ref, x1_ref, h2_ref, acc_ref, *, seq_len, final):
    mod = mod_ref[0]
    g1 = mod[:, 2 * D_MODEL:3 * D_MODEL]
    sh2 = mod[:, 3 * D_MODEL:4 * D_MODEL]
    sc2 = mod[:, 4 * D_MODEL:5 * D_MODEL]
    g2 = mod[:, 5 * D_MODEL:6 * D_MODEL]
    x1 = x_ref[...] + g1 * _dot(y_ref[...], wo_ref[...])
    x1_ref[...] = x1
    ms = jnp.mean(x1 * x1, axis=-1, keepdims=True)
    h2_ref[...] = (x1 * lax.rsqrt(ms + EPS) * n2_ref[...] * (1.0 + sc2) + sh2).astype(BF16)
    acc_ref[...] = jnp.zeros_like(acc_ref)

    def ff_chunk(j, carry):
        h2 = h2_ref[...]
        g = _dot(h2, wg_ref[j])
        u = _dot(h2, wu_ref[j])
        cw = cw_ref[j]
        gc = (cw[0:1] * _shift_rows(g, 1, seq_len) + cw[1:2] * g
              + cw[2:3] * _shift_rows(g, -1, seq_len) + cb_ref[j])
        act = (_silu(gc) * u).astype(BF16)
        acc_ref[...] += _dot(act, wd_ref[j])
        return carry

    lax.fori_loop(0, N_FF, ff_chunk, 0)
    x2 = x1_ref[...] + g2 * acc_ref[...]
    if final:
        ms2 = jnp.mean(x2 * x2, axis=-1, keepdims=True)
        x2 = x2 * lax.rsqrt(ms2 + EPS) * fw_ref[...]
    o_ref[...] = x2


def _outffn_call(x, ymix, mod3, lw, final_w, *, tile, seq_len, cond_base, tiles_per_cond, final):
    rows = x.shape[0]
    const2 = lambda i: (0, 0)
    const3 = lambda i: (0, 0, 0)
    one = pl.Buffered(1)
    return pl.pallas_call(
        functools.partial(_outffn_kernel, seq_len=seq_len, final=final),
        grid=(rows // tile,),
        in_specs=[
            pl.BlockSpec((tile, D_MODEL), lambda i: (i, 0)),
            pl.BlockSpec((tile, D_MODEL), lambda i: (i, 0)),
            pl.BlockSpec((1, 1, 6 * D_MODEL), lambda i: (cond_base + i // tiles_per_cond, 0, 0)),
            pl.BlockSpec((1, D_MODEL), const2),
            pl.BlockSpec((D_MODEL, D_MODEL), const2, pipeline_mode=one),
            pl.BlockSpec((N_FF, D_MODEL, FF_CHUNK), const3, pipeline_mode=one),
            pl.BlockSpec((N_FF, D_MODEL, FF_CHUNK), const3, pipeline_mode=one),
            pl.BlockSpec((N_FF, 3, FF_CHUNK), const3),
            pl.BlockSpec((N_FF, 1, FF_CHUNK), const3),
            pl.BlockSpec((N_FF, FF_CHUNK, D_MODEL), const3, pipeline_mode=one),
            pl.BlockSpec((1, D_MODEL), const2),
        ],
        out_specs=pl.BlockSpec((tile, D_MODEL), lambda i: (i, 0)),
        out_shape=jax.ShapeDtypeStruct((rows, D_MODEL), F32),
        scratch_shapes=[
            pltpu.VMEM((tile, D_MODEL), F32),
            pltpu.VMEM((tile, D_MODEL), BF16),
            pltpu.VMEM((tile, D_MODEL), F32),
        ],
        compiler_params=pltpu.CompilerParams(
            dimension_semantics=("arbitrary",), vmem_limit_bytes=VMEM_LIMIT),
        name="out_ffn",
    )(x, ymix, mod3, lw["norm2_w"], lw["w_out"], lw["ffn_wg"], lw["ffn_wu"], lw["ffn_cw"],
      lw["ffn_cb"], lw["ffn_wd"], final_w)


def _chunk_dir(q, k, v, g, dec, ecol, tail, elast, s_in):
    vb = v.astype(BF16)
    y = _dot((g * dec).astype(BF16), vb)
    y = y + _dot((q * ecol).astype(BF16), s_in.astype(BF16))
    s_out = elast * s_in + _dot_tn((k * tail).astype(BF16), vb)
    return y, s_out


def _cmul(ar, ai, br, bi):
    return ar * br - ai * bi, ar * bi + ai * br


def _tile_scan_const(br, bi, pw, ptab, cr, ci, fwd):
    row = lax.broadcasted_iota(jnp.int32, br.shape, 0)
    for d, (pr, pi) in zip((1, 2, 4), pw):
        if fwd:
            keep = row >= d
            sr = jnp.where(keep, pltpu.roll(br, d, axis=0), 0.0)
            si = jnp.where(keep, pltpu.roll(bi, d, axis=0), 0.0)
        else:
            keep = row < SUB - d
            sr = jnp.where(keep, pltpu.roll(br, SUB - d, axis=0), 0.0)
            si = jnp.where(keep, pltpu.roll(bi, SUB - d, axis=0), 0.0)
        tr, ti = _cmul(pr, pi, sr, si)
        br, bi = br + tr, bi + ti
    tr, ti = _cmul(ptab[0], ptab[1], cr, ci)
    return br + tr, bi + ti


def _tile_scan_var(a, b, c, fwd):
    row = lax.broadcasted_iota(jnp.int32, a.shape, 0)
    for d in (1, 2, 4):
        if fwd:
            keep = row >= d
            sa = jnp.where(keep, pltpu.roll(a, d, axis=0), 1.0)
            sb = jnp.where(keep, pltpu.roll(b, d, axis=0), 0.0)
        else:
            keep = row < SUB - d
            sa = jnp.where(keep, pltpu.roll(a, SUB - d, axis=0), 1.0)
            sb = jnp.where(keep, pltpu.roll(b, SUB - d, axis=0), 0.0)
        b = b + a * sb
        a = a * sa
    return b + a * c


def _mixer_kernel(*refs, L, rotary, emit_state):
    it = iter(refs)
    p_ref, v512, v256, v128, v1024 = (next(it) for _ in range(5))
    bre_ref, bim_ref, cre_ref, cim_ref, glu_ref, wa_ref, wx_ref = (next(it) for _ in range(7))
    if rotary:
        cos_ref, sin_ref = next(it), next(it)
    i_ssd, i_ret, i_s5, i_lru = (next(it) for _ in range(4))
    o_ref = next(it)
    if emit_state:
        n_ssd, n_ret, n_s5, n_lru = (next(it) for _ in range(4))
    xbc_s, dt_s, la_s, y_s, q_s, k_s, a_s, u_s = (next(it) for _ in range(8))

    nc = L // T
    li = lax.broadcasted_iota(jnp.int32, (T, T), 0)
    si = lax.broadcasted_iota(jnp.int32, (T, T), 1)
    lower = li >= si
    upper = si >= li
    tril = lower.astype(F32)
    triu = upper.astype(F32)

    xbc_raw = p_ref[:, OFF_XBC:OFF_XBC + SSD_XBC]
    acc = v512[4:5, :] + v512[2:3, :] * xbc_raw
    for j in (0, 1, 3):
        acc = acc + v512[j:j + 1, :] * _shift_rows(xbc_raw, 2 - j, L)
    xbc_s[...] = _silu(acc)
    dt_all = _softplus(p_ref[:, OFF_DT:OFF_DT + 128] + v128[0:1, :])
    dt_s[...] = dt_all
    la_s[...] = dt_all * (-jnp.exp(v128[1:2, :]))

    def ssd_chunk(c, fwd, states):
        r0 = pl.multiple_of(c * T, T)
        xbc = xbc_s[pl.ds(r0, T), :]
        dt = dt_s[pl.ds(r0, T), :]
        la = la_s[pl.ds(r0, T), :]
        cum = jnp.dot(tril if fwd else triu, la, preferred_element_type=F32, precision=HIGHEST)
        cum_t = cum.T
        last_row = T - 1 if fwd else 0
        col0 = 0 if fwd else SSD_HEADS
        gs = []
        for grp in range(2):
            bm = xbc[:, 256 + 64 * grp:320 + 64 * grp]
            cm = xbc[:, 384 + 64 * grp:448 + 64 * grp]
            gs.append((_dot_nt(cm.astype(BF16), bm.astype(BF16)), bm, cm))
        ys, new_states = [], []
        for h in range(SSD_HEADS):
            g, bm, cm = gs[h // 2]
            cc = col0 + h
            col = cum[:, cc:cc + 1]
            row = cum_t[cc:cc + 1, :]
            last = cum[last_row:last_row + 1, cc:cc + 1]
            dec = jnp.where(lower if fwd else upper, jnp.exp(col - row), 0.0)
            v = xbc[:, 64 * h:64 * h + 64] * dt[:, cc:cc + 1]
            y, s_new = _chunk_dir(cm, bm, v, g, dec, jnp.exp(col), jnp.exp(last - col),
                                  jnp.exp(last), states[h])
            ys.append(y)
            new_states.append(s_new)
        return r0, jnp.concatenate(ys, axis=1), tuple(new_states)

    def ssd_fwd(c, states):
        r0, y, states = ssd_chunk(c, True, states)
        y_s[pl.ds(r0, T), :] = y
        return states

    def ssd_bwd(i, states):
        r0, y, states = ssd_chunk(nc - 1 - i, False, states)
        y_s[pl.ds(r0, T), :] += y
        return states

    sf = lax.fori_loop(0, nc, ssd_fwd, tuple(i_ssd[0, 0, h] for h in range(SSD_HEADS)))
    sb = lax.fori_loop(0, nc, ssd_bwd, tuple(i_ssd[0, 1, h] for h in range(SSD_HEADS)))
    if emit_state:
        for h in range(SSD_HEADS):
            n_ssd[0, 0, h] = sf[h]
            n_ssd[0, 1, h] = sb[h]
    x_ssd = xbc_s[:, 0:MIX_W]
    y = (y_s[...] + v256[0:1, :] * x_ssd) * _silu(p_ref[:, OFF_Z:OFF_Z + MIX_W])
    ms = jnp.mean(y * y, axis=-1, keepdims=True)
    o_ref[:, 0:MIX_W] = (y * lax.rsqrt(ms + EPS) * v256[1:2, :]).astype(BF16)

    q = p_ref[:, OFF_Q:OFF_Q + MIX_W]
    k = p_ref[:, OFF_K:OFF_K + MIX_W]
    if rotary:
        lane = lax.broadcasted_iota(jnp.int32, (L, MIX_W), 1)
        is_re = (lane % 32) < 16
        cos = cos_ref[...]
        sin = sin_ref[...]

        def rot(t):
            sw = jnp.where(is_re, pltpu.roll(t, MIX_W - 16, axis=1), pltpu.roll(t, 16, axis=1))
            return t * cos + sw * sin

        q = rot(q)
        k = rot(k)
    q_s[...] = q
    k_s[...] = k * (RET_HEAD_DIM ** -0.5)

    tcol = lax.broadcasted_iota(jnp.int32, (T, 1), 0).astype(F32)
    dist_f = (li - si).astype(F32)
    dist_b = (si - li).astype(F32)

    def ret_chunk(c, fwd, states):
        r0 = pl.multiple_of(c * T, T)
        qc = q_s[pl.ds(r0, T), :]
        kc = k_s[pl.ds(r0, T), :]
        vc = p_ref[pl.ds(r0, T), OFF_V:OFF_V + MIX_W]
        ys, new_states = [], []
        for h in range(RET_HEADS):
            lg = math.log1p(-2.0 ** (-5.0 - (0.0 if fwd else 0.5) - h))
            qh = qc[:, 64 * h:64 * h + 64]
            kh = kc[:, 64 * h:64 * h + 64]
            vh = vc[:, 64 * h:64 * h + 64]
            g = _dot_nt(qh.astype(BF16), kh.astype(BF16))
            if fwd:
                dec = jnp.where(lower, jnp.exp(dist_f * lg), 0.0)
                ecol = jnp.exp((tcol + 1.0) * lg)
                tail = jnp.exp((T - 1.0 - tcol) * lg)
            else:
                dec = jnp.where(upper, jnp.exp(dist_b * lg), 0.0)
                ecol = jnp.exp((T - tcol) * lg)
                tail = jnp.exp(tcol * lg)
            y, s_new = _chunk_dir(qh, kh, vh, g, dec, ecol, tail, math.exp(T * lg), states[h])
            ys.append(y)
            new_states.append(s_new)
        return r0, jnp.concatenate(ys, axis=1), tuple(new_states)

    def ret_fwd(c, states):
        r0, y, states = ret_chunk(c, True, states)
        y_s[pl.ds(r0, T), :] = y
        return states

    def ret_bwd(i, states):
        r0, y, states = ret_chunk(nc - 1 - i, False, states)
        y_s[pl.ds(r0, T), :] += y
        return states

    rf = lax.fori_loop(0, nc, ret_fwd, tuple(i_ret[0, 0, h] for h in range(RET_HEADS)))
    rb = lax.fori_loop(0, nc, ret_bwd, tuple(i_ret[0, 1, h] for h in range(RET_HEADS)))
    if emit_state:
        for h in range(RET_HEADS):
            n_ret[0, 0, h] = rf[h]
            n_ret[0, 1, h] = rb[h]
    yr = y_s[...]
    parts = []
    for h in range(RET_HEADS):
        yh = yr[:, 64 * h:64 * h + 64]
        mu = jnp.mean(yh, axis=-1, keepdims=True)
        d = yh - mu
        var = jnp.mean(d * d, axis=-1, keepdims=True)
        parts.append(d * lax.rsqrt(var + EPS))
    yn = jnp.concatenate(parts, axis=1) * v256[2:3, :]
    o_ref[:, MIX_W:2 * MIX_W] = (_silu(p_ref[:, OFF_G:OFF_G + MIX_W]) * yn).astype(BF16)

    def s5_dir(dr, fwd):
        step = jnp.exp(v1024[4 + dr:5 + dr, :])
        lr = v1024[dr:dr + 1, :]
        lim = v1024[2 + dr:3 + dr, :]
        mag = jnp.exp(lr * step)
        ar = mag * jnp.cos(lim * step)
        ai = mag * jnp.sin(lim * step)
        den = lr * lr + lim * lim
        nr, ni = ar - 1.0, ai
        cr = (nr * lr + ni * lim) / den
        ci = (ni * lr - nr * lim) / den
        p1 = (ar, ai)
        p2 = _cmul(*p1, *p1)
        p3 = _cmul(*p2, *p1)
        p4 = _cmul(*p2, *p2)
        p5 = _cmul(*p4, *p1)
        p6 = _cmul(*p4, *p2)
        p7 = _cmul(*p4, *p3)
        p8 = _cmul(*p4, *p4)
        pows = (p1, p2, p3, p4, p5, p6, p7, p8)
        rowi = lax.broadcasted_iota(jnp.int32, (SUB, S5_W), 0)
        ptab = []
        for part in range(2):
            tab = jnp.zeros((SUB, S5_W), F32)
            for r in range(SUB):
                e = r if fwd else SUB - 1 - r
                tab = jnp.where(rowi == r, pows[e][part], tab)
            ptab.append(tab)
        pw = (p1, p2, p4)

        def chunk(i, carry):
            c = i if fwd else nc - 1 - i
            r0 = pl.multiple_of(c * T, T)
            ub = p_ref[pl.ds(r0, T), OFF_S5:OFF_S5 + MIX_W].astype(BF16)
            pm = _dot(ub, bre_ref[...])
            qm = _dot(ub, bim_ref[...])
            bur = cr * pm - ci * qm
            bui = cr * qm + ci * pm
            xr, xi = [None] * (T // SUB), [None] * (T // SUB)
            car, cai = carry
            order = range(T // SUB) if fwd else range(T // SUB - 1, -1, -1)
            for t in order:
                tr, ti = _tile_scan_const(bur[SUB * t:SUB * t + SUB], bui[SUB * t:SUB * t + SUB],
                                          pw, ptab, car, cai, fwd)
                xr[t], xi[t] = tr, ti
                edge = SUB - 1 if fwd else 0
                car, cai = tr[edge:edge + 1], ti[edge:edge + 1]
            xr = jnp.concatenate(xr, axis=0).astype(BF16)
            xi = jnp.concatenate(xi, axis=0).astype(BF16)
            yc = _dot(xr, cre_ref[...]) - _dot(xi, cim_ref[...])
            if fwd:
                y_s[pl.ds(r0, T), :] = yc
            else:
                y_s[pl.ds(r0, T), :] += yc
            return car, cai

        return lax.fori_loop(0, nc, chunk, (i_s5[0, dr, 0:1, :], i_s5[0, dr, 1:2, :]))

    s5f = s5_dir(0, True)
    s5b = s5_dir(1, False)
    if emit_state:
        n_s5[0, 0, 0:1, :] = s5f[0]
        n_s5[0, 0, 1:2, :] = s5f[1]
        n_s5[0, 1, 0:1, :] = s5b[0]
        n_s5[0, 1, 1:2, :] = s5b[1]
    u5 = p_ref[:, OFF_S5:OFF_S5 + MIX_W]
    y5 = y_s[...] + v256[3:4, :] * u5
    yg = _dot(y5.astype(BF16), glu_ref[...])
    o_ref[:, 2 * MIX_W:3 * MIX_W] = (yg[:, 0:MIX_W] * jax.nn.sigmoid(yg[:, MIX_W:2 * MIX_W])).astype(BF16)

    xb_raw = p_ref[:, OFF_LXB:OFF_LXB + MIX_W]
    xb = v256[8:9, :] + v256[6:7, :] * xb_raw
    for j in (0, 1, 3):
        xb = xb + v256[4 + j:5 + j, :] * _shift_rows(xb_raw, 2 - j, L)
    xbb = xb.astype(BF16)
    nt = L // SUB
    unroll = 8

    def lru_dir(dr, fwd):
        r = jax.nn.sigmoid(_dot(xbb, wa_ref[dr]) + v256[9 + dr:10 + dr, :])
        gi = jax.nn.sigmoid(_dot(xbb, wx_ref[dr]) + v256[11 + dr:12 + dr, :])
        log_a = -LRU_C * r * _softplus(-v256[13 + dr:14 + dr, :])
        a = jnp.exp(log_a)
        a_s[...] = a
        u_s[...] = jnp.sqrt(-jnp.tanh(log_a) * (a * a + 1.0)) * (gi * xb)

        def body(i, carry):
            for jj in range(unroll):
                t = i * unroll + jj
                if not fwd:
                    t = nt - 1 - t
                r0 = pl.multiple_of(t * SUB, SUB)
                h = _tile_scan_var(a_s[pl.ds(r0, SUB), :], u_s[pl.ds(r0, SUB), :], carry, fwd)
                if fwd:
                    y_s[pl.ds(r0, SUB), :] = h
                    carry = h[SUB - 1:SUB]
                else:
                    y_s[pl.ds(r0, SUB), :] += h
                    carry = h[0:1]
            return carry

        return lax.fori_loop(0, nt // unroll, body, i_lru[0, dr:dr + 1, :])

    lf = lru_dir(0, True)
    lb = lru_dir(1, False)
    if emit_state:
        n_lru[0, 0:1, :] = lf
        n_lru[0, 1:2, :] = lb
    o_ref[:, 3 * MIX_W:4 * MIX_W] = (
        y_s[...] * jax.nn.gelu(p_ref[:, OFF_LGB:OFF_LGB + MIX_W])).astype(BF16)


def _mixer_call(proj, lw, rot_tabs, init, *, nseq, L, rotary, emit_state):
    full = lambda a: pl.BlockSpec(a.shape, lambda b, _n=a.ndim: (0,) * _n)
    per_seq = lambda a: pl.BlockSpec((1,) + a.shape[1:], lambda b, _n=a.ndim: (b,) + (0,) * (_n - 1))
    params = [lw[k] for k in ("v512", "v256", "v128", "v1024", "s5_bre", "s5_bim", "s5_cre", "s5_cim",
                              "s5_glu", "lru_wa", "lru_wx")]
    inputs = [proj] + params + (list(rot_tabs) if rotary else []) + list(init)
    in_specs = ([pl.BlockSpec((L, IN_PAD), lambda b: (b, 0))] + [full(a) for a in params]
                + ([full(a) for a in rot_tabs] if rotary else []) + [per_seq(a) for a in init])
    out_shape = [jax.ShapeDtypeStruct((nseq * L, D_MODEL), BF16)]
    out_specs = [pl.BlockSpec((L, D_MODEL), lambda b: (b, 0))]
    if emit_state:
        for a in init:
            out_shape.append(jax.ShapeDtypeStruct(a.shape, F32))
            out_specs.append(per_seq(a))
    scratch = [
        pltpu.VMEM((L, SSD_XBC), F32), pltpu.VMEM((L, 128), F32), pltpu.VMEM((L, 128), F32),
        pltpu.VMEM((L, MIX_W), F32), pltpu.VMEM((L, MIX_W), F32), pltpu.VMEM((L, MIX_W), F32),
        pltpu.VMEM((L, MIX_W), F32), pltpu.VMEM((L, MIX_W), F32),
    ]
    return pl.pallas_call(
        functools.partial(_mixer_kernel, L=L, rotary=rotary, emit_state=emit_state),
        grid=(nseq,),
        in_specs=in_specs,
        out_specs=out_specs,
        out_shape=out_shape,
        scratch_shapes=scratch,
        compiler_params=pltpu.CompilerParams(
            dimension_semantics=("arbitrary",), vmem_limit_bytes=VMEM_LIMIT),
        name="mixers_rot" if rotary else "mixers",
    )(*inputs)


def _block_diag(blocks):
    n, r, c = blocks.shape
    eye = jnp.eye(n, dtype=blocks.dtype)
    return (blocks[:, :, None, :] * eye[:, None, :, None]).reshape(n * r, n * c)


def _pad_rows(a, rows):
    return jnp.concatenate([a, jnp.zeros((rows - a.shape[0],) + a.shape[1:], a.dtype)], axis=0)


def _pad_lanes(a, lanes):
    return jnp.concatenate([a, jnp.zeros(a.shape[:-1] + (lanes - a.shape[-1],), a.dtype)], axis=-1)


def _rotary_tables(L):
    nf = RET_HEAD_DIM // 4
    pos = jnp.arange(L, dtype=jnp.int32)
    rowcol = jnp.stack([pos // GRID_W, pos % GRID_W], axis=-1).astype(F32)
    inv = ROPE_BASE ** (-jnp.arange(nf, dtype=F32) / nf)
    ang = rowcol[:, :, None] * inv
    cos = jnp.cos(ang)
    sin = jnp.sin(ang)
    cos_t = jnp.stack([cos, cos], axis=2).reshape(L, RET_HEAD_DIM)
    sin_t = jnp.stack([-sin, sin], axis=2).reshape(L, RET_HEAD_DIM)
    return jnp.tile(cos_t, (1, RET_HEADS)), jnp.tile(sin_t, (1, RET_HEADS))


def kernel(x_prompt, x_sample, state_ssd, state_ret, state_s5, state_lru, c, c_ctx, norm1_w, norm2_w, mod_w, mod_b, w_in, ssd_conv_w, ssd_conv_b, ssd_dt_bias, ssd_a_log, ssd_d, ssd_norm_w, ret_gn_w, s5_lam_re, s5_lam_im, s5_log_step, s5_b_re, s5_b_im, s5_c_re, s5_c_im, s5_d, s5_glu_w, lru_conv_w, lru_conv_b, lru_wa, lru_ba, lru_wx, lru_bx, lru_lambda, w_out, ffn_w_up, ffn_conv_w, ffn_conv_b, ffn_w_down, final_norm_w):
    bp, lp, _ = x_prompt.shape
    bs, ls, _ = x_sample.shape
    cond = jnp.concatenate([c.astype(F32), c_ctx.astype(F32)[None, :],
                            jnp.zeros((N_COND - bs - 1, D_MODEL), F32)], axis=0)
    mod_all = _mod_call(cond, mod_w, mod_b)
    rot_tabs = _rotary_tables(ls)
    final_w = final_norm_w.reshape(1, D_MODEL)

    xp = x_prompt.astype(F32).reshape(bp * lp, D_MODEL)
    xs = x_sample.astype(F32).reshape(bs * ls, D_MODEL)
    zero_init = (jnp.zeros((bp, 2, SSD_HEADS, SSD_STATE, SSD_HEAD_DIM), F32),
                 jnp.zeros((bp, 2, RET_HEADS, RET_HEAD_DIM, RET_HEAD_DIM), F32),
                 jnp.zeros((bp, 2, 2, S5_W), F32),
                 jnp.zeros((bp, 2, MIX_W), F32))
    new_states = [[], [], [], []]
    ffn_tile = 1024
    for l in range(DEPTH):
        wl = w_in[l]
        lw = {
            "w_in": jnp.concatenate([wl[:, :768], wl[:, 776:], _pad_lanes(wl[:, 768:776], 128)],
                                    axis=1).astype(BF16),
            "norm2_w": norm2_w[l].reshape(1, D_MODEL),
            "w_out": w_out[l].astype(BF16),
            "ffn_wg": ffn_w_up[l][:, :D_FF].reshape(D_MODEL, N_FF, FF_CHUNK).transpose(1, 0, 2).astype(BF16),
            "ffn_wu": ffn_w_up[l][:, D_FF:].reshape(D_MODEL, N_FF, FF_CHUNK).transpose(1, 0, 2).astype(BF16),
            "ffn_cw": ffn_conv_w[l].reshape(3, N_FF, FF_CHUNK).transpose(1, 0, 2),
            "ffn_cb": ffn_conv_b[l].reshape(N_FF, 1, FF_CHUNK),
            "ffn_wd": ffn_w_down[l].reshape(N_FF, FF_CHUNK, D_MODEL).astype(BF16),
            "v512": _pad_rows(jnp.concatenate([ssd_conv_w[l], ssd_conv_b[l][None]], axis=0), 8),
            "v256": _pad_rows(jnp.concatenate([
                jnp.repeat(ssd_d[l], SSD_HEAD_DIM)[None], ssd_norm_w[l][None], ret_gn_w[l][None],
                s5_d[l][None], lru_conv_w[l], lru_conv_b[l][None],
                lru_ba[l].reshape(2, MIX_W), lru_bx[l].reshape(2, MIX_W), lru_lambda[l]], axis=0), 16),
            "v128": _pad_rows(jnp.stack([_pad_lanes(ssd_dt_bias[l].reshape(8), 128),
                                         _pad_lanes(ssd_a_log[l].reshape(8), 128)], axis=0), 8),
            "v1024": _pad_rows(jnp.concatenate([
                s5_lam_re[l].reshape(2, S5_W), s5_lam_im[l].reshape(2, S5_W),
                jnp.repeat(s5_log_step[l], S5_STATE, axis=-1)], axis=0), 8),
            "s5_bre": _block_diag(s5_b_re[l].transpose(0, 2, 1)).astype(BF16),
            "s5_bim": _block_diag(s5_b_im[l].transpose(0, 2, 1)).astype(BF16),
            "s5_cre": _block_diag(s5_c_re[l].transpose(0, 2, 1)).astype(BF16),
            "s5_cim": _block_diag(s5_c_im[l].transpose(0, 2, 1)).astype(BF16),
            "s5_glu": s5_glu_w[l].astype(BF16),
            "lru_wa": jnp.stack([_block_diag(lru_wa[l, d]) for d in range(2)]).astype(BF16),
            "lru_wx": jnp.stack([_block_diag(lru_wx[l, d]) for d in range(2)]).astype(BF16),
        }
        mod3 = mod_all[l].reshape(N_COND, 1, 6 * D_MODEL)
        n1 = norm1_w[l].reshape(1, D_MODEL)

        pp = _inproj_call(xp, mod3, n1, lw["w_in"], tile=256, cond_base=CTX_ROW,
                          tiles_per_cond=bp * lp // 256)
        ps = _inproj_call(xs, mod3, n1, lw["w_in"], tile=256, cond_base=0, tiles_per_cond=ls // 256)

        outs = _mixer_call(pp, lw, rot_tabs, zero_init, nseq=bp, L=lp, rotary=False, emit_state=True)
        ymp = outs[0]
        new_states[0].append(outs[1])
        new_states[1].append(outs[2])
        new_states[2].append(outs[3].reshape(bp, 2, 2, S5_GROUPS, S5_STATE).transpose(0, 1, 3, 4, 2))
        new_states[3].append(outs[4])
        cache = (state_ssd[:, l].astype(F32), state_ret[:, l].astype(F32),
                 state_s5[:, l].astype(F32).transpose(0, 1, 4, 2, 3).reshape(bs, 2, 2, S5_W),
                 state_lru[:, l].astype(F32))
        yms = _mixer_call(ps, lw, rot_tabs, cache, nseq=bs, L=ls, rotary=True, emit_state=False)[0]

        final = l == DEPTH - 1
        xp = _outffn_call(xp, ymp, mod3, lw, final_w, tile=ffn_tile, seq_len=lp, cond_base=CTX_ROW,
                          tiles_per_cond=bp * lp // ffn_tile, final=final)
        xs = _outffn_call(xs, yms, mod3, lw, final_w, tile=ffn_tile, seq_len=ls, cond_base=0,
                          tiles_per_cond=ls // ffn_tile, final=final)

    return (xp.reshape(bp, lp, D_MODEL), xs.reshape(bs, ls, D_MODEL),
            jnp.stack(new_states[0], axis=1), jnp.stack(new_states[1], axis=1),
            jnp.stack(new_states[2], axis=1), jnp.stack(new_states[3], axis=1))
```
